```python
import math
import jax, jax.numpy as jnp
from jax import lax
import numpy as np

D_MODEL = 1024
BATCH = 8
SEQ = 4096
DEPTH = 1
DEC_BATCH = 128
DEC_SEQ = 8
PAST_LEN = 8192
PAGE_SIZE = 128

SSD_HEADS = 16
SSD_HEAD_DIM = 64
SSD_INNER = SSD_HEADS * SSD_HEAD_DIM
SSD_GROUPS = 2
SSD_STATE = 128
SSD_GN = SSD_GROUPS * SSD_STATE
SSD_CONV = 4
SSD_CONV_CH = SSD_INNER + 2 * SSD_GN
SSD_CHUNK = 128
FOX_HEADS = 16
FOX_KV_HEADS = 4
FOX_HEAD_DIM = 64
FOX_Q_PER_KV = FOX_HEADS // FOX_KV_HEADS
FOX_WIDTH = FOX_HEADS * FOX_HEAD_DIM
FOX_Q_BLOCK = 128
D_FF = ((8 * D_MODEL // 3 + 127) // 128) * 128
FFN_CONV = 3
IN_SIZES = (SSD_INNER, SSD_INNER, SSD_GN, SSD_GN, SSD_HEADS,
            FOX_WIDTH, FOX_KV_HEADS * FOX_HEAD_DIM, FOX_KV_HEADS * FOX_HEAD_DIM, FOX_HEADS)
D_IN_PROJ = 2 * SSD_INNER + 2 * SSD_GN + SSD_HEADS + (FOX_HEADS + 2 * FOX_KV_HEADS) * FOX_HEAD_DIM + FOX_HEADS
EPS = 1e-6
NEG_INF = -1e30

kernel_name = 'hybrid_ssd_fox_convffn_adaln_step'


def rms_norm(x, g):
    xf = x.astype(jnp.float32)
    xf = xf * lax.rsqrt(jnp.mean(xf * xf, axis=-1, keepdims=True) + EPS)
    return (xf * g.astype(jnp.float32)).astype(x.dtype)


def split_cols(t, sizes):
    cuts, acc = [], 0
    for s in sizes[:-1]:
        acc += s
        cuts.append(acc)
    return jnp.split(t, cuts, axis=-1)


def causal_dwconv(x, buf, w, b):
    width, T = w.shape[0], x.shape[1]
    xp = jnp.concatenate([buf.astype(x.dtype), x], axis=1)
    out = b + xp[:, 0:T] * w[0]
    for j in range(1, width):
        out = out + xp[:, j:j + T] * w[j]
    return out, xp[:, T:]


def ssd_scan(x, dt, A, Bm, Cm, h0):
    f32 = jnp.float32
    b, T, H, P = x.shape
    G, N = Bm.shape[2], Bm.shape[3]
    R = H // G
    L = SSD_CHUNK if T % SSD_CHUNK == 0 else T
    nc = T // L
    xc = x.reshape(b, nc, L, G, R, P).astype(f32)
    dtc = dt.reshape(b, nc, L, G, R)
    Bc = Bm.reshape(b, nc, L, G, N).astype(f32)
    Cc = Cm.reshape(b, nc, L, G, N).astype(f32)
    acum = jnp.cumsum(dtc * A.reshape(G, R), axis=2)
    causal = jnp.tril(jnp.ones((L, L), dtype=bool))[:, :, None, None]
    seg = acum[:, :, :, None] - acum[:, :, None, :]
    decay = jnp.exp(jnp.where(causal, seg, -jnp.inf))
    CB = jnp.einsum('bclgn,bcsgn->bclsg', Cc, Bc)
    y_diag = jnp.einsum('bclsg,bclsgr,bcsgr,bcsgrp->bclgrp', CB, decay, dtc, xc)
    decay_end = jnp.exp(acum[:, :, -1:] - acum)
    chunk_states = jnp.einsum('bcsgn,bcsgr,bcsgrp->bcgrpn', Bc, decay_end * dtc, xc)
    chunk_decay = jnp.exp(acum[:, :, -1])

    def step(h, inp):
        s, d = inp
        return d[..., None, None] * h + s, h

    h_final, h_in = lax.scan(step, h0.reshape(b, G, R, P, N).astype(f32),
                             (jnp.moveaxis(chunk_states, 1, 0), jnp.moveaxis(chunk_decay, 1, 0)))
    h_in = jnp.moveaxis(h_in, 0, 1)
    y_off = jnp.einsum('bclgn,bcgrpn,bclgr->bclgrp', Cc, h_in, jnp.exp(acum))
    y = (y_diag + y_off).reshape(b, T, H, P)
    return y, h_final.reshape(b, H, P, N)


def fox_attend(q, k, v, Fq, Fk, q_pos, k_pos):
    s = jnp.einsum('bqkgd,bskd->bkgqs', q, k).astype(jnp.float32) * (FOX_HEAD_DIM ** -0.5)
    s = s + (jnp.transpose(Fq, (0, 2, 3, 1))[..., :, None] - jnp.transpose(Fk, (0, 2, 3, 1))[..., None, :])
    s = jnp.where(k_pos[None, :] <= q_pos[:, None], s, NEG_INF)
    p = jax.nn.softmax(s, axis=-1)
    return jnp.einsum('bkgqs,bskd->bqkgd', p.astype(v.dtype), v)


def trunk_layer(x, c, ssd_buf, ssd_h0, ffn_buf, past, p):
    f32 = jnp.float32
    b, T, _ = x.shape
    mod = jax.nn.silu(c) @ p['w_ada'] + p['b_ada']
    sh1, sc1, g1, sh2, sc2, g2 = jnp.split(mod[:, None, :], 6, axis=-1)
    h = rms_norm(x, p['g_pre_mix']) * (1 + sc1) + sh1
    z, xs, Bm, Cm, dt_raw, q, k, v, f_raw = split_cols(h @ p['w_in'], IN_SIZES)

    xBC, new_ssd_buf = causal_dwconv(jnp.concatenate([xs, Bm, Cm], axis=-1), ssd_buf,
                                     p['ssd_conv_w'], p['ssd_conv_b'])
    xs, Bm, Cm = split_cols(jax.nn.silu(xBC), (SSD_INNER, SSD_GN, SSD_GN))
    dt = jax.nn.softplus(dt_raw.astype(f32) + p['ssd_dt_bias'].astype(f32))
    A = -jnp.exp(p['ssd_A_log'].astype(f32))
    xh = xs.reshape(b, T, SSD_HEADS, SSD_HEAD_DIM)
    y, h_T = ssd_scan(xh, dt, A, Bm.reshape(b, T, SSD_GROUPS, SSD_STATE),
                      Cm.reshape(b, T, SSD_GROUPS, SSD_STATE), ssd_h0)
    y = y + p['ssd_D'].astype(f32)[:, None] * xh.astype(f32)
    y = y.reshape(b, T, SSD_GROUPS, SSD_INNER // SSD_GROUPS) * \
        jax.nn.silu(z.astype(f32)).reshape(b, T, SSD_GROUPS, SSD_INNER // SSD_GROUPS)
    y = y * lax.rsqrt(jnp.mean(y * y, axis=-1, keepdims=True) + EPS)
    y = y.reshape(b, T, SSD_INNER) * p['ssd_norm_g'].astype(f32)
    y_ssd = y.astype(x.dtype) @ p['w_ssd_out']

    logf = jax.nn.log_sigmoid(f_raw.astype(f32) + p['fox_b_f'].astype(f32))
    qh = q.reshape(b, T, FOX_KV_HEADS, FOX_Q_PER_KV, FOX_HEAD_DIM)
    kh = k.reshape(b, T, FOX_KV_HEADS, FOX_HEAD_DIM)
    vh = v.reshape(b, T, FOX_KV_HEADS, FOX_HEAD_DIM)
    if past is None:
        F = jnp.cumsum(logf, axis=1).reshape(b, T, FOX_KV_HEADS, FOX_Q_PER_KV)
        nb = T // FOX_Q_BLOCK
        qb = jnp.swapaxes(qh.reshape(b, nb, FOX_Q_BLOCK, FOX_KV_HEADS, FOX_Q_PER_KV, FOX_HEAD_DIM), 0, 1)
        Fqb = jnp.swapaxes(F.reshape(b, nb, FOX_Q_BLOCK, FOX_KV_HEADS, FOX_Q_PER_KV), 0, 1)
        posb = jnp.arange(T).reshape(nb, FOX_Q_BLOCK)
        kpos = jnp.arange(T)
        ob = lax.map(lambda blk: fox_attend(blk[0], kh, vh, blk[1], F, blk[2], kpos), (qb, Fqb, posb))
        o = jnp.swapaxes(ob, 0, 1)
    else:
        k_past, v_past, logf_past = past
        S0 = k_past.shape[1]
        k_all = jnp.concatenate([k_past.astype(kh.dtype), kh], axis=1)
        v_all = jnp.concatenate([v_past.astype(vh.dtype), vh], axis=1)
        F_all = jnp.cumsum(jnp.concatenate([logf_past.astype(f32), logf], axis=1), axis=1)
        F_all = F_all.reshape(b, S0 + T, FOX_KV_HEADS, FOX_Q_PER_KV)
        o = fox_attend(qh, k_all, v_all, F_all[:, S0:], F_all, S0 + jnp.arange(T), jnp.arange(S0 + T))
    y_fox = o.reshape(b, T, FOX_WIDTH).astype(x.dtype) @ p['w_fox_out']

    ga, gb = jnp.split(jax.nn.sigmoid(h @ p['w_merge'] + p['b_merge']), 2, axis=-1)
    mix = (ga * y_ssd + gb * y_fox) @ p['w_o']
    x = x + g1 * rms_norm(mix, p['g_post_mix'])

    h2 = rms_norm(x, p['g_pre_ffn']) * (1 + sc2) + sh2
    up, gate = jnp.split(h2 @ p['w_ffn_in'], 2, axis=-1)
    up_conv, new_ffn_buf = causal_dwconv(up, ffn_buf, p['ffn_conv_w'], p['ffn_conv_b'])
    f = (jax.nn.gelu(up_conv) * gate) @ p['w_ffn_out']
    x = x + g2 * rms_norm(f, p['g_post_ffn'])
    return x, (kh, vh, logf.astype(x.dtype), new_ssd_buf, h_T.astype(x.dtype), new_ffn_buf)


def setup_inputs(seed: int = 0) -> dict:
    key = jax.random.key(seed)
    keys = list(jax.random.split(key, 48))
    cnt = [0]

    def nk():
        cnt[0] += 1
        return keys[cnt[0] - 1]

    def nrm(shape, scale):
        return jax.random.normal(nk(), shape, jnp.float32) * scale

    L = DEPTH
    n_pages = PAST_LEN // PAGE_SIZE
    n_pool = (DEC_BATCH * n_pages * 5) // 4
    page_table = jax.random.permutation(nk(), n_pool)[:DEC_BATCH * n_pages]
    page_table = page_table.reshape(DEC_BATCH, n_pages).astype(jnp.int32)
    dt0 = jnp.exp(jax.random.uniform(nk(), (L, SSD_HEADS), jnp.float32,
                                     minval=math.log(1e-3), maxval=math.log(1e-1)))
    dt_bias = dt0 + jnp.log(-jnp.expm1(-dt0))
    return {
        'x_prompt': nrm((BATCH, SEQ, D_MODEL), 1.0),
        'x_sample': nrm((DEC_BATCH, DEC_SEQ, D_MODEL), 1.0),
        'cache_k': nrm((L, n_pool, PAGE_SIZE, FOX_KV_HEADS, FOX_HEAD_DIM), 1.0),
        'cache_v': nrm((L, n_pool, PAGE_SIZE, FOX_KV_HEADS, FOX_HEAD_DIM), 1.0),
        'cache_logf': jax.nn.log_sigmoid(jax.random.uniform(nk(), (L, n_pool, PAGE_SIZE, FOX_HEADS),
                                                            jnp.float32, minval=1.0, maxval=6.0)),
        'state_ssd_conv': nrm((L, DEC_BATCH, SSD_CONV - 1, SSD_CONV_CH), 1.0),
        'state_ssd': nrm((L, DEC_BATCH, SSD_HEADS, SSD_HEAD_DIM, SSD_STATE), 0.3),
        'state_ffn_conv': nrm((L, DEC_BATCH, FFN_CONV - 1, D_FF), 1.0),
        'page_table': page_table,
        'c_prompt': nrm((BATCH, D_MODEL), 1.0),
        'c_sample': nrm((DEC_BATCH, D_MODEL), 1.0),
        'w_ada': nrm((L, D_MODEL, 6 * D_MODEL), 0.5 * D_MODEL ** -0.5),
        'b_ada': nrm((L, 6 * D_MODEL), 0.01),
        'g_pre_mix': 1.0 + nrm((L, D_MODEL), 0.01),
        'w_in': nrm((L, D_MODEL, D_IN_PROJ), D_MODEL ** -0.5),
        'ssd_conv_w': nrm((L, SSD_CONV, SSD_CONV_CH), SSD_CONV ** -0.5),
        'ssd_conv_b': nrm((L, SSD_CONV_CH), 0.01),
        'ssd_dt_bias': dt_bias,
        'ssd_A_log': jnp.log(jax.random.uniform(nk(), (L, SSD_HEADS), jnp.float32, minval=1.0, maxval=16.0)),
        'ssd_D': 1.0 + nrm((L, SSD_HEADS), 0.01),
        'ssd_norm_g': 1.0 + nrm((L, SSD_INNER), 0.01),
        'w_ssd_out': nrm((L, SSD_INNER, D_MODEL), SSD_INNER ** -0.5),
        'fox_b_f': jax.random.uniform(nk(), (L, FOX_HEADS), jnp.float32, minval=1.0, maxval=6.0),
        'w_fox_out': nrm((L, FOX_WIDTH, D_MODEL), FOX_WIDTH ** -0.5),
        'w_merge': nrm((L, D_MODEL, 2 * D_MODEL), D_MODEL ** -0.5),
        'b_merge': nrm((L, 2 * D_MODEL), 0.01),
        'w_o': nrm((L, D_MODEL, D_MODEL), D_MODEL ** -0.5),
        'g_post_mix': 1.0 + nrm((L, D_MODEL), 0.01),
        'g_pre_ffn': 1.0 + nrm((L, D_MODEL), 0.01),
        'w_ffn_in': nrm((L, D_MODEL, 2 * D_FF), D_MODEL ** -0.5),
        'ffn_conv_w': nrm((L, FFN_CONV, D_FF), FFN_CONV ** -0.5),
        'ffn_conv_b': nrm((L, D_FF), 0.01),
        'w_ffn_out': nrm((L, D_FF, D_MODEL), D_FF ** -0.5),
        'g_post_ffn': 1.0 + nrm((L, D_MODEL), 0.01),
    }


def reference(x_prompt, x_sample, cache_k, cache_v, cache_logf, state_ssd_conv, state_ssd, state_ffn_conv,
              page_table, c_prompt, c_sample, w_ada, b_ada, g_pre_mix, w_in, ssd_conv_w, ssd_conv_b,
              ssd_dt_bias, ssd_A_log, ssd_D, ssd_norm_g, w_ssd_out, fox_b_f, w_fox_out, w_merge, b_merge,
              w_o, g_post_mix, g_pre_ffn, w_ffn_in, ffn_conv_w, ffn_conv_b, w_ffn_out, g_post_ffn):
    weights = dict(w_ada=w_ada, b_ada=b_ada, g_pre_mix=g_pre_mix, w_in=w_in, ssd_conv_w=ssd_conv_w,
                   ssd_conv_b=ssd_conv_b, ssd_dt_bias=ssd_dt_bias, ssd_A_log=ssd_A_log, ssd_D=ssd_D,
                   ssd_norm_g=ssd_norm_g, w_ssd_out=w_ssd_out, fox_b_f=fox_b_f, w_fox_out=w_fox_out,
                   w_merge=w_merge, b_merge=b_merge, w_o=w_o, g_post_mix=g_post_mix, g_pre_ffn=g_pre_ffn,
                   w_ffn_in=w_ffn_in, ffn_conv_w=ffn_conv_w, ffn_conv_b=ffn_conv_b, w_ffn_out=w_ffn_out,
                   g_post_ffn=g_post_ffn)
    bp = x_prompt.shape[0]
    db, n_pages = page_table.shape
    page = cache_k.shape[2]
    yp, ys = x_prompt, x_sample
    outs_p = [[] for _ in range(6)]
    outs_s = [[] for _ in range(6)]
    for l in range(DEPTH):
        p = {name: w[l] for name, w in weights.items()}
        ssd_buf0 = jnp.zeros((bp, SSD_CONV - 1, SSD_CONV_CH), x_prompt.dtype)
        ssd_h0 = jnp.zeros((bp, SSD_HEADS, SSD_HEAD_DIM, SSD_STATE), jnp.float32)
        ffn_buf0 = jnp.zeros((bp, FFN_CONV - 1, D_FF), x_prompt.dtype)
        yp, st_p = trunk_layer(yp, c_prompt, ssd_buf0, ssd_h0, ffn_buf0, None, p)
        k_past = cache_k[l][page_table].reshape(db, n_pages * page, FOX_KV_HEADS, FOX_HEAD_DIM)
        v_past = cache_v[l][page_table].reshape(db, n_pages * page, FOX_KV_HEADS, FOX_HEAD_DIM)
        lf_past = cache_logf[l][page_table].reshape(db, n_pages * page, FOX_HEADS)
        ys, st_s = trunk_layer(ys, c_sample, state_ssd_conv[l], state_ssd[l], state_ffn_conv[l],
                               (k_past, v_past, lf_past), p)
        for i in range(6):
            outs_p[i].append(st_p[i])
            outs_s[i].append(st_s[i])
    k_p, v_p, lf_p, sc_p, ss_p, fc_p = [jnp.stack(o) for o in outs_p]
    k_s, v_s, lf_s, sc_s, ss_s, fc_s = [jnp.stack(o) for o in outs_s]
    return (yp, ys, k_p, v_p, lf_p, sc_p, ss_p, fc_p, k_s, v_s, lf_s, sc_s, ss_s, fc_s)
```

```python
import functools

import jax
import jax.numpy as jnp
from jax import lax
from jax.experimental import pallas as pl
from jax.experimental.pallas import tpu as pltpu

F32 = jnp.float32
BF16 = jnp.bfloat16
EPS = 1e-6
NEG_INF = -1e30
HIGHEST = lax.Precision.HIGHEST

LANES = 128
SUBLANES = 8
VMEM_LIMIT = 56 * 1024 * 1024

SSD_HEAD_DIM = 64
SSD_GROUPS = 2
SSD_STATE = 128
SSD_CONV = 4
FOX_HEAD_DIM = 64
FOX_Q_PER_KV = 4
FFN_CONV = 3
CONV_PAD = 8

NT_DIMS = (((1,), (1,)), ((), ()))
TN_DIMS = (((0,), (0,)), ((), ()))


def _dot(a, b):
    return jnp.dot(a, b, preferred_element_type=F32)


def _dot_nt(a, b, precision=None):
    return lax.dot_general(a, b, NT_DIMS, precision=precision, preferred_element_type=F32)


def _dot_tn(a, b):
    return lax.dot_general(a, b, TN_DIMS, preferred_element_type=F32)


def _dot_hi(a, b):
    return jnp.dot(a, b, precision=HIGHEST, preferred_element_type=F32)


def _sigmoid(x):
    return 1.0 / (1.0 + jnp.exp(-x))


def _silu(x):
    return x * _sigmoid(x)


def _softplus(x):
    return jnp.maximum(x, 0.0) + jnp.log(1.0 + jnp.exp(-jnp.abs(x)))


def _log_sigmoid(x):
    return jnp.minimum(x, 0.0) - jnp.log(1.0 + jnp.exp(-jnp.abs(x)))


def _gelu_tanh(x):
    c = 0.7978845608028654
    return 0.5 * x * (1.0 + jnp.tanh(c * (x + 0.044715 * (x * x * x))))


def _rms(x, g):
    ms = jnp.mean(x * x, axis=-1, keepdims=True)
    return x * lax.rsqrt(ms + EPS) * g


def _params(sem):
    return pltpu.CompilerParams(dimension_semantics=sem, vmem_limit_bytes=VMEM_LIMIT)


def _const_spec(shape):
    n = len(shape)
    return pl.BlockSpec(shape, lambda *_: (0,) * n, pipeline_mode=pl.Buffered(1))


def _ada_kernel(c_ref, w_ref, b_ref, o_ref):
    s = _silu(c_ref[...]).astype(BF16)
    o_ref[...] = _dot(s, w_ref[...].astype(BF16)) + b_ref[...]


def _ada_call(c_all, w_ada, b_ada):
    n, d = c_all.shape
    dn = w_ada.shape[1]
    tn = 1024
    return pl.pallas_call(
        _ada_kernel,
        grid=(dn // tn,),
        in_specs=[pl.BlockSpec((n, d), lambda j: (0, 0)),
                  pl.BlockSpec((d, tn), lambda j: (0, j)),
                  pl.BlockSpec((1, tn), lambda j: (0, j))],
        out_specs=pl.BlockSpec((n, tn), lambda j: (0, j)),
        out_shape=jax.ShapeDtypeStruct((n, dn), F32),
        compiler_params=_params(("arbitrary",)),
        name="ada",
    )(c_all, w_ada, b_ada.reshape(1, dn))


def _inproj_kernel(x_ref, sc_ref, sh_ref, g_ref, wz, wxbc, wdt, wq, wk, wv, wf, wmg, bmg, bf_ref,
                   z_o, xbc_o, dt_o, q_o, k_o, v_o, lf_o, ga_o, gb_o):
    nb, tt, d = x_ref.shape
    h = _rms(x_ref[...], g_ref[...]) * (1.0 + sc_ref[...]) + sh_ref[...]
    hb = h.reshape(nb * tt, d).astype(BF16)

    def proj(w):
        return _dot(hb, w[...])

    z_o[...] = proj(wz).reshape(z_o.shape)
    xbc_o[...] = proj(wxbc).reshape(xbc_o.shape)
    dt_o[...] = proj(wdt).reshape(dt_o.shape)
    q_o[...] = (proj(wq) * (FOX_HEAD_DIM ** -0.5)).astype(BF16).reshape(q_o.shape)
    k_o[...] = proj(wk).reshape(k_o.shape)
    v_o[...] = proj(wv).reshape(v_o.shape)
    nh = lf_o.shape[-1]
    lf = _log_sigmoid(proj(wf)[:, :nh] + bf_ref[...])
    lf_o[...] = lf.reshape(lf_o.shape)
    gates = _sigmoid(proj(wmg) + bmg[...])
    dm = ga_o.shape[-1]
    ga_o[...] = gates[:, :dm].reshape(ga_o.shape)
    gb_o[...] = gates[:, dm:].reshape(gb_o.shape)


def _inproj_call(x3, sc, sh, g_pre, w, nb, tt):
    bt, t, d = x3.shape
    grid = (bt // nb, t // tt)
    n_z, n_xbc, n_q, n_kv = w["wz"].shape[1], w["wxbc"].shape[1], w["wq"].shape[1], w["wk"].shape[1]
    nh = w["bf"].shape[1]
    dm = w["wmg"].shape[1] // 2

    def row_spec(n):
        return pl.BlockSpec((nb, tt, n), lambda i, j: (i, j, 0))

    mod_spec = pl.BlockSpec((nb, 1, d), lambda i, j: (i, 0, 0))
    consts = [g_pre, w["wz"], w["wxbc"], w["wdt"], w["wq"], w["wk"], w["wv"], w["wf"], w["wmg"], w["bmg"], w["bf"]]
    out_dims = [(n_z, F32), (n_xbc, F32), (LANES, F32), (n_q, BF16), (n_kv, F32), (n_kv, F32), (nh, F32),
                (dm, F32), (dm, F32)]
    return pl.pallas_call(
        _inproj_kernel,
        grid=grid,
        in_specs=[row_spec(d), mod_spec, mod_spec] + [_const_spec(c.shape) for c in consts],
        out_specs=[row_spec(n) for n, _ in out_dims],
        out_shape=[jax.ShapeDtypeStruct((bt, t, n), dt) for n, dt in out_dims],
        compiler_params=_params(("arbitrary", "arbitrary")),
        name="inproj",
    )(x3, sc, sh, *consts)


def _fcum_kernel(lf_ref, f_o, ft_o, carry):
    t = lf_ref.shape[1]
    nh = lf_ref.shape[2]

    @pl.when(pl.program_id(1) == 0)
    def _():
        carry[...] = jnp.zeros_like(carry)

    lf = lf_ref[0]
    r = lax.broadcasted_iota(jnp.int32, (t, t), 0)
    c = lax.broadcasted_iota(jnp.int32, (t, t), 1)
    tril = (r >= c).astype(F32)
    f = _dot_hi(tril, lf) + carry[...]
    f_o[0] = f
    eye = (lax.broadcasted_iota(jnp.int32, (nh, nh), 0) == lax.broadcasted_iota(jnp.int32, (nh, nh), 1)).astype(F32)
    ft_o[0] = _dot_nt(eye, f, precision=HIGHEST)
    carry[...] = f[t - 1:t, :]


def _fcum_call(lf, tt):
    b, t, nh = lf.shape
    return pl.pallas_call(
        _fcum_kernel,
        grid=(b, t // tt),
        in_specs=[pl.BlockSpec((1, tt, nh), lambda i, j: (i, j, 0))],
        out_specs=[pl.BlockSpec((1, tt, nh), lambda i, j: (i, j, 0)),
                   pl.BlockSpec((1, nh, tt), lambda i, j: (i, 0, j))],
        out_shape=[jax.ShapeDtypeStruct((b, t, nh), F32), jax.ShapeDtypeStruct((b, nh, t), F32)],
        scratch_shapes=[pltpu.VMEM((1, nh), F32)],
        compiler_params=_params(("arbitrary", "arbitrary")),
        name="fcum",
    )(lf)


def _ssd_kernel(xbc_ref, z_ref, dt_ref, buf_ref, h0_ref, cw_ref, cb_ref, dtb_ref, a_ref, d_ref, ng_ref,
                y_o, cs_o, hf_o, xp, st):
    L = xbc_ref.shape[1]
    n_inner = z_ref.shape[2]
    n_state = SSD_STATE
    n_pairs = st.shape[0]
    pairs_per_group = n_pairs // SSD_GROUPS

    @pl.when(pl.program_id(1) == 0)
    def _():
        xp[0:CONV_PAD, :] = buf_ref[0]
        st[...] = h0_ref[0]

    xp[CONV_PAD:CONV_PAD + L, :] = xbc_ref[0]
    acc = cb_ref[...] + cw_ref[0:1, :] * xp[pl.ds(CONV_PAD - (SSD_CONV - 1), L), :]
    for j in range(1, SSD_CONV):
        acc = acc + cw_ref[j:j + 1, :] * xp[pl.ds(CONV_PAD - (SSD_CONV - 1) + j, L), :]
    tail = xp[pl.ds(L, CONV_PAD), :]
    xp[0:CONV_PAD, :] = tail
    cs_o[0] = tail
    xbc = _silu(acc)
    xs = xbc[:, :n_inner]

    dt = _softplus(dt_ref[0] + dtb_ref[...])
    a = dt * a_ref[...]
    r = lax.broadcasted_iota(jnp.int32, (L, L), 0)
    c = lax.broadcasted_iota(jnp.int32, (L, L), 1)
    causal = r >= c
    acum = _dot_hi(causal.astype(F32), a)
    eye = (lax.broadcasted_iota(jnp.int32, (LANES, LANES), 0)
           == lax.broadcasted_iota(jnp.int32, (LANES, LANES), 1)).astype(F32)
    a_t = _dot_nt(eye, a, precision=HIGHEST)
    dt_t = _dot_nt(eye, dt, precision=HIGHEST)
    acum_t = _dot_hi(a_t, (r <= c).astype(F32))
    atot = acum[L - 1:L, :]
    ea = jnp.exp(acum)
    de = jnp.exp(atot - acum) * dt

    lane = lax.broadcasted_iota(jnp.int32, (L, LANES), 1)
    first = lane < SSD_HEAD_DIM
    row_first = lax.broadcasted_iota(jnp.int32, (LANES, n_state), 0) < SSD_HEAD_DIM

    ys = []
    for g in range(SSD_GROUPS):
        bg = xbc[:, n_inner + g * n_state:n_inner + (g + 1) * n_state].astype(BF16)
        cg = xbc[:, n_inner + (SSD_GROUPS + g) * n_state:n_inner + (SSD_GROUPS + g + 1) * n_state].astype(BF16)
        cb = _dot_nt(cg, bg)
        for pp in range(pairs_per_group):
            pr = g * pairs_per_group + pp
            x_pair = xs[:, pr * LANES:(pr + 1) * LANES]
            xb = x_pair.astype(BF16)
            yd = []
            for hh in (2 * pr, 2 * pr + 1):
                seg = acum[:, hh:hh + 1] - acum_t[hh:hh + 1, :]
                decay = jnp.exp(jnp.where(causal, seg, -jnp.inf))
                m = cb * decay * dt_t[hh:hh + 1, :]
                yd.append(_dot(m.astype(BF16), xb))
            y_diag = jnp.where(first, yd[0], yd[1])
            s_pair = st[pr]
            y_off = _dot_nt(cg, s_pair.astype(BF16))
            h0, h1 = 2 * pr, 2 * pr + 1
            ea_pair = jnp.where(first, ea[:, h0:h0 + 1], ea[:, h1:h1 + 1])
            ys.append(y_diag + y_off * ea_pair)
            w_pair = jnp.where(first, de[:, h0:h0 + 1], de[:, h1:h1 + 1])
            s_add = _dot_tn((x_pair * w_pair).astype(BF16), bg)
            cd = jnp.exp(jnp.where(row_first, acum_t[h0:h0 + 1, L - 1:L], acum_t[h1:h1 + 1, L - 1:L]))
            st[pr] = cd * s_pair + s_add
    hf_o[0] = st[...]

    y = jnp.concatenate(ys, axis=1) + d_ref[...] * xs
    y = y * _silu(z_ref[0])
    gw = n_inner // SSD_GROUPS
    yn = []
    for g in range(SSD_GROUPS):
        yg = y[:, g * gw:(g + 1) * gw]
        yn.append(yg * lax.rsqrt(jnp.mean(yg * yg, axis=-1, keepdims=True) + EPS))
    y_o[0] = (jnp.concatenate(yn, axis=1) * ng_ref[...]).astype(BF16)


def _ssd_call(xbc, z, dtr, buf8, h0p, p, L):
    bt, t, n_xbc = xbc.shape
    n_inner = z.shape[2]
    n_pairs = h0p.shape[1]
    consts = [p["cw"], p["cb"], p["dtb"], p["a"], p["d"], p["ng"]]

    def seq_spec(n):
        return pl.BlockSpec((1, L, n), lambda i, j: (i, j, 0))

    return pl.pallas_call(
        _ssd_kernel,
        grid=(bt, t // L),
        in_specs=[seq_spec(n_xbc), seq_spec(n_inner), seq_spec(LANES),
                  pl.BlockSpec((1, CONV_PAD, n_xbc), lambda i, j: (i, 0, 0)),
                  pl.BlockSpec((1, n_pairs, LANES, SSD_STATE), lambda i, j: (i, 0, 0, 0))]
                 + [_const_spec(c.shape) for c in consts],
        out_specs=[seq_spec(n_inner),
                   pl.BlockSpec((1, CONV_PAD, n_xbc), lambda i, j: (i, 0, 0)),
                   pl.BlockSpec((1, n_pairs, LANES, SSD_STATE), lambda i, j: (i, 0, 0, 0))],
        out_shape=[jax.ShapeDtypeStruct((bt, t, n_inner), BF16),
                   jax.ShapeDtypeStruct((bt, CONV_PAD, n_xbc), F32),
                   jax.ShapeDtypeStruct(h0p.shape, F32)],
        scratch_shapes=[pltpu.VMEM((CONV_PAD + L, n_xbc), F32),
                        pltpu.VMEM((n_pairs, LANES, SSD_STATE), F32)],
        compiler_params=_params(("arbitrary", "arbitrary")),
        name="ssd",
    )(xbc, z, dtr, buf8, h0p, *consts)


def _flash_kernel(q_ref, k_ref, v_ref, f_ref, ft_ref, o_ref, m_s, l_s, acc_s):
    tq = q_ref.shape[1]
    tk = k_ref.shape[1]
    n_heads = q_ref.shape[2] // FOX_HEAD_DIM
    qi = pl.program_id(1)
    kj = pl.program_id(2)

    @pl.when(kj == 0)
    def _():
        m_s[...] = jnp.full_like(m_s, NEG_INF)
        l_s[...] = jnp.zeros_like(l_s)
        acc_s[...] = jnp.zeros_like(acc_s)

    @pl.when(kj <= qi)
    def _():
        q_pos = qi * tq + lax.broadcasted_iota(jnp.int32, (tq, tk), 0)
        k_pos = kj * tk + lax.broadcasted_iota(jnp.int32, (tq, tk), 1)
        visible = k_pos <= q_pos
        q = q_ref[0]
        k = k_ref[0].astype(BF16)
        v = v_ref[0].astype(BF16)
        fq = f_ref[0]
        fk = ft_ref[0]
        for h in range(n_heads):
            g = h // FOX_Q_PER_KV
            qh = q[:, h * FOX_HEAD_DIM:(h + 1) * FOX_HEAD_DIM]
            kh = k[:, g * FOX_HEAD_DIM:(g + 1) * FOX_HEAD_DIM]
            vh = v[:, g * FOX_HEAD_DIM:(g + 1) * FOX_HEAD_DIM]
            s = _dot_nt(qh, kh) + (fq[:, h:h + 1] - fk[h:h + 1, :])
            s = jnp.where(visible, s, NEG_INF)
            m_old = m_s[h]
            m_new = jnp.maximum(m_old, jnp.max(s, axis=-1, keepdims=True))
            alpha = jnp.exp(m_old - m_new)
            p = jnp.exp(s - m_new[:, :1])
            l_s[h] = alpha * l_s[h] + jnp.sum(p, axis=-1, keepdims=True)
            m_s[h] = m_new
            sl = slice(h * FOX_HEAD_DIM, (h + 1) * FOX_HEAD_DIM)
            acc_s[:, sl] = alpha[:, :FOX_HEAD_DIM] * acc_s[:, sl] + _dot(p.astype(BF16), vh)

    @pl.when(kj == qi)
    def _():
        outs = []
        for h in range(n_heads):
            sl = slice(h * FOX_HEAD_DIM, (h + 1) * FOX_HEAD_DIM)
            outs.append(acc_s[:, sl] / l_s[h][:, :FOX_HEAD_DIM])
        o_ref[0] = jnp.concatenate(outs, axis=1).astype(o_ref.dtype)


def _flash_call(q, k, v, f, ft, tq):
    b, t, nq = q.shape
    nkv = k.shape[2]
    nh = f.shape[2]
    nblk = t // tq

    def kv_idx(i, a, c):
        return (i, jnp.minimum(a, c), 0)

    return pl.pallas_call(
        _flash_kernel,
        grid=(b, nblk, nblk),
        in_specs=[pl.BlockSpec((1, tq, nq), lambda i, a, c: (i, a, 0)),
                  pl.BlockSpec((1, tq, nkv), kv_idx),
                  pl.BlockSpec((1, tq, nkv), kv_idx),
                  pl.BlockSpec((1, tq, nh), lambda i, a, c: (i, a, 0)),
                  pl.BlockSpec((1, nh, tq), lambda i, a, c: (i, 0, jnp.minimum(a, c)))],
        out_specs=pl.BlockSpec((1, tq, nq), lambda i, a, c: (i, a, 0)),
        out_shape=jax.ShapeDtypeStruct((b, t, nq), BF16),
        scratch_shapes=[pltpu.VMEM((nh, tq, LANES), F32), pltpu.VMEM((nh, tq, LANES), F32),
                        pltpu.VMEM((tq, nq), F32)],
        compiler_params=_params(("arbitrary", "arbitrary", "arbitrary")),
        name="flash",
    )(q, k, v, f, ft)


def _paged_kernel(pt_ref, q_ref, kn_ref, vn_ref, lfn_ref, *rest, pages_per_step):
    del pt_ref
    P = pages_per_step
    k_refs, v_refs, lf_refs = rest[:P], rest[P:2 * P], rest[2 * P:3 * P]
    o_ref = rest[3 * P]
    qbd, knp, vnp, lfp, m_s, l_s, acc_s, carry, qn_s = rest[3 * P + 1:]
    tq = q_ref.shape[1]
    n_heads = q_ref.shape[2] // FOX_HEAD_DIM
    n_kv = n_heads // FOX_Q_PER_KV
    rows = n_heads * tq
    kvw = n_kv * FOX_HEAD_DIM
    page = k_refs[0].shape[2]
    step = pl.program_id(1)

    row_head = lax.broadcasted_iota(jnp.int32, (rows, n_heads), 0) // tq
    expand = (row_head == lax.broadcasted_iota(jnp.int32, (rows, n_heads), 1)).astype(F32)
    pr = lax.broadcasted_iota(jnp.int32, (page, page), 0)
    pc = lax.broadcasted_iota(jnp.int32, (page, page), 1)

    @pl.when(step == 0)
    def _():
        q = q_ref[0].astype(F32)
        lane = lax.broadcasted_iota(jnp.int32, (tq, kvw), 1)
        blocks = []
        for h in range(n_heads):
            g, j = h // FOX_Q_PER_KV, h % FOX_Q_PER_KV
            qg = q[:, g * kvw:(g + 1) * kvw]
            shift = ((g - j) * FOX_HEAD_DIM) % kvw
            rolled = pltpu.roll(qg, shift, 1) if shift else qg
            blocks.append(jnp.where((lane >= g * FOX_HEAD_DIM) & (lane < (g + 1) * FOX_HEAD_DIM), rolled, 0.0))
        qbd[...] = jnp.concatenate(blocks, axis=0).astype(BF16)

        knp[...] = jnp.zeros_like(knp)
        vnp[...] = jnp.zeros_like(vnp)
        lfp[...] = jnp.zeros_like(lfp)
        knp[0:tq, :] = kn_ref[0]
        vnp[0:tq, :] = vn_ref[0]
        lfp[0:tq, :] = lfn_ref[0]
        le = _dot_nt(expand, lfp[...], precision=HIGHEST)
        cn = _dot_hi(le, (pr <= pc).astype(F32))
        col = lax.broadcasted_iota(jnp.int32, (rows, page), 1)
        qidx = lax.broadcasted_iota(jnp.int32, (rows, page), 0) % tq
        qn = jnp.sum(jnp.where(col == qidx, cn, 0.0), axis=-1, keepdims=True)
        s = _dot_nt(qbd[...], knp[...].astype(BF16)) + (qn - cn)
        s = jnp.where(col <= qidx, s, NEG_INF)
        m = jnp.max(s, axis=-1, keepdims=True)
        p = jnp.exp(s - m)
        m_s[...] = jnp.broadcast_to(m, m_s.shape)
        l_s[...] = jnp.broadcast_to(jnp.sum(p, axis=-1, keepdims=True), l_s.shape)
        acc_s[...] = _dot(p.astype(BF16), vnp[...].astype(BF16))
        qn_s[...] = jnp.broadcast_to(qn, qn_s.shape)
        carry[...] = jnp.zeros_like(carry)

    suffix = (pr > pc).astype(F32)
    for i in range(P):
        lf_t = lf_refs[i][0]
        le = _dot_hi(expand, lf_t)
        rloc = _dot_hi(le, suffix)
        bias = qn_s[...] + carry[...] + rloc
        s = _dot(qbd[...], k_refs[i][0].astype(BF16)) + bias
        m_old = m_s[...]
        m_new = jnp.maximum(m_old, jnp.max(s, axis=-1, keepdims=True))
        alpha = jnp.exp(m_old - m_new)
        p = jnp.exp(s - m_new)
        l_s[...] = alpha * l_s[...] + jnp.sum(p, axis=-1, keepdims=True)
        m_s[...] = m_new
        acc_s[...] = jnp.concatenate([alpha] * (kvw // LANES), axis=1) * acc_s[...] + _dot_nt(
            p.astype(BF16), v_refs[i][0].astype(BF16))
        carry[...] = carry[...] + jnp.sum(le, axis=-1, keepdims=True)

    @pl.when(step == pl.num_programs(1) - 1)
    def _():
        o = acc_s[...] / jnp.concatenate([l_s[...]] * (kvw // LANES), axis=1)
        lane = lax.broadcasted_iota(jnp.int32, (tq, kvw), 1)
        groups = []
        for g in range(n_kv):
            tot = jnp.zeros((tq, kvw), F32)
            for j in range(FOX_Q_PER_KV):
                h = g * FOX_Q_PER_KV + j
                blk = o[h * tq:(h + 1) * tq, :]
                shift = ((j - g) * FOX_HEAD_DIM) % kvw
                rolled = pltpu.roll(blk, shift, 1) if shift else blk
                tot = tot + jnp.where((lane >= j * FOX_HEAD_DIM) & (lane < (j + 1) * FOX_HEAD_DIM), rolled, 0.0)
            groups.append(tot)
        o_ref[0] = jnp.concatenate(groups, axis=1).astype(o_ref.dtype)


def _paged_call(q, kn, vn, lfn, cache_k, cache_v, cache_lf, page_table, pages_per_step):
    b, tq, nq = q.shape
    n_pool, kvw, page = cache_k.shape
    nh = cache_lf.shape[1]
    n_pages = page_table.shape[1]
    P = pages_per_step
    steps = n_pages // P
    rows = nh * tq

    def new_spec(n):
        return pl.BlockSpec((1, tq, n), lambda i, s, pt: (i, 0, 0))

    def page_spec(n, slot):
        return pl.BlockSpec((1, n, page), lambda i, s, pt: (pt[i, n_pages - 1 - (s * P + slot)], 0, 0))

    grid_spec = pltpu.PrefetchScalarGridSpec(
        num_scalar_prefetch=1,
        grid=(b, steps),
        in_specs=[new_spec(nq), new_spec(kvw), new_spec(kvw), new_spec(nh)]
                 + [page_spec(kvw, i) for i in range(P)]
                 + [page_spec(kvw, i) for i in range(P)]
                 + [page_spec(nh, i) for i in range(P)],
        out_specs=pl.BlockSpec((1, tq, nq), lambda i, s, pt: (i, 0, 0)),
        scratch_shapes=[pltpu.VMEM((rows, kvw), BF16),
                        pltpu.VMEM((page, kvw), F32), pltpu.VMEM((page, kvw), F32), pltpu.VMEM((page, nh), F32),
                        pltpu.VMEM((rows, LANES), F32), pltpu.VMEM((rows, LANES), F32),
                        pltpu.VMEM((rows, kvw), F32),
                        pltpu.VMEM((rows, LANES), F32), pltpu.VMEM((rows, LANES), F32)],
    )
    return pl.pallas_call(
        functools.partial(_paged_kernel, pages_per_step=P),
        grid_spec=grid_spec,
        out_shape=jax.ShapeDtypeStruct((b, tq, nq), BF16),
        compiler_params=_params(("arbitrary", "arbitrary")),
        name="paged",
    )(page_table, q, kn, vn, lfn, *([cache_k] * P), *([cache_v] * P), *([cache_lf] * P))


def _post_kernel(x_ref, ys_ref, o_ref, ga_ref, gb_ref, g1_ref, sc2_ref, sh2_ref,
                 wso, wfo, wo, gpost, gpre, x1_o, h2_o):
    nb, tt, d = x_ref.shape
    rows = nb * tt
    y_ssd = _dot(ys_ref[...].reshape(rows, -1), wso[...])
    y_fox = _dot(o_ref[...].reshape(rows, -1), wfo[...])
    mix_in = ga_ref[...].reshape(rows, d) * y_ssd + gb_ref[...].reshape(rows, d) * y_fox
    mix = _dot(mix_in.astype(BF16), wo[...]).reshape(nb, tt, d)
    x1 = x_ref[...] + g1_ref[...] * _rms(mix, gpost[...])
    x1_o[...] = x1
    h2 = _rms(x1, gpre[...]) * (1.0 + sc2_ref[...]) + sh2_ref[...]
    h2_o[...] = h2.astype(BF16)


def _post_call(x3, ys, o, ga, gb, g1, sc2, sh2, w, nb, tt):
    bt, t, d = x3.shape
    n_inner, n_fox = ys.shape[2], o.shape[2]

    def row_spec(n):
        return pl.BlockSpec((nb, tt, n), lambda i, j: (i, j, 0))

    mod_spec = pl.BlockSpec((nb, 1, d), lambda i, j: (i, 0, 0))
    consts = [w["wso"], w["wfo"], w["wo"], w["gpost"], w["gpre"]]
    return pl.pallas_call(
        _post_kernel,
        grid=(bt // nb, t // tt),
        in_specs=[row_spec(d), row_spec(n_inner), row_spec(n_fox), row_spec(d), row_spec(d),
                  mod_spec, mod_spec, mod_spec] + [_const_spec(c.shape) for c in consts],
        out_specs=[row_spec(d), row_spec(d)],
        out_shape=[jax.ShapeDtypeStruct((bt, t, d), F32), jax.ShapeDtypeStruct((bt, t, d), BF16)],
        compiler_params=_params(("arbitrary", "arbitrary")),
        name="post",
    )(x3, ys, o, ga, gb, g1, sc2, sh2, *consts)


def _ffn_kernel(x1_ref, h2_ref, g2_ref, buf_ref, wi, cw_ref, cb_ref, wout, gpost, y_o, cs_o, xp):
    nb, tt, d = x1_ref.shape
    dff = cw_ref.shape[1]
    rows = nb * tt

    @pl.when(pl.program_id(1) == 0)
    def _():
        xp[:, 0:CONV_PAD, :] = buf_ref[...]

    ug = _dot(h2_ref[...].reshape(rows, d), wi[...])
    xp[:, CONV_PAD:CONV_PAD + tt, :] = ug[:, :dff].reshape(nb, tt, dff)
    conv = cb_ref[...] + cw_ref[0:1, :] * xp[:, pl.ds(CONV_PAD - (FFN_CONV - 1), tt), :]
    for j in range(1, FFN_CONV):
        conv = conv + cw_ref[j:j + 1, :] * xp[:, pl.ds(CONV_PAD - (FFN_CONV - 1) + j, tt), :]
    tail = xp[:, pl.ds(tt, CONV_PAD), :]
    xp[:, 0:CONV_PAD, :] = tail
    cs_o[...] = tail
    act = _gelu_tanh(conv).reshape(rows, dff) * ug[:, dff:]
    f = _dot(act.astype(BF16), wout[...]).reshape(nb, tt, d)
    y_o[...] = x1_ref[...] + g2_ref[...] * _rms(f, gpost[...])


def _ffn_call(x1, h2, g2, buf8, w, nb, tt):
    bt, t, d = x1.shape
    dff = w["cw"].shape[1]

    def row_spec(n):
        return pl.BlockSpec((nb, tt, n), lambda i, j: (i, j, 0))

    consts = [w["wi"], w["cw"], w["cb"], w["wout"], w["gpost"]]
    return pl.pallas_call(
        _ffn_kernel,
        grid=(bt // nb, t // tt),
        in_specs=[row_spec(d), row_spec(d), pl.BlockSpec((nb, 1, d), lambda i, j: (i, 0, 0)),
                  pl.BlockSpec((nb, CONV_PAD, dff), lambda i, j: (i, 0, 0))]
                 + [_const_spec(c.shape) for c in consts],
        out_specs=[row_spec(d), pl.BlockSpec((nb, CONV_PAD, dff), lambda i, j: (i, 0, 0))],
        out_shape=[jax.ShapeDtypeStruct((bt, t, d), F32), jax.ShapeDtypeStruct((bt, CONV_PAD, dff), F32)],
        scratch_shapes=[pltpu.VMEM((nb, CONV_PAD + tt, dff), F32)],
        compiler_params=_params(("arbitrary", "arbitrary")),
        name="ffn",
    )(x1, h2, g2, buf8, *consts)


def _pad_hist(buf):
    return jnp.pad(buf, ((0, 0), (CONV_PAD - buf.shape[1], 0), (0, 0)))


def _pad_lanes(v):
    return jnp.pad(v.astype(F32), (0, LANES - v.shape[0])).reshape(1, LANES)


def _prep_weights(p):
    d = p["w_in"].shape[0]
    n_heads = p["ssd_dt_bias"].shape[0]
    n_inner = n_heads * SSD_HEAD_DIM
    n_gn = SSD_GROUPS * SSD_STATE
    nf = p["fox_b_f"].shape[0]
    n_fox = nf * FOX_HEAD_DIM
    n_kv = n_fox // FOX_Q_PER_KV
    cuts = [n_inner, n_inner + 2 * n_gn, n_heads, n_fox, n_kv, n_kv, nf]
    offs = [0]
    for c in cuts:
        offs.append(offs[-1] + c)
    w_in = p["w_in"]
    cols = [w_in[:, offs[i]:offs[i + 1]] for i in range(len(cuts))]

    def pad_cols(w):
        return jnp.pad(w, ((0, 0), (0, LANES - w.shape[1])))

    bf = lambda w: w.astype(BF16)
    row = lambda v: v.astype(F32).reshape(1, -1)
    inproj = dict(wz=bf(cols[0]), wxbc=bf(cols[1]), wdt=bf(pad_cols(cols[2])), wq=bf(cols[3]), wk=bf(cols[4]),
                  wv=bf(cols[5]), wf=bf(pad_cols(cols[6])), wmg=bf(p["w_merge"]), bmg=row(p["b_merge"]),
                  bf=row(p["fox_b_f"]))
    ssd = dict(cw=p["ssd_conv_w"].astype(F32), cb=row(p["ssd_conv_b"]), dtb=_pad_lanes(p["ssd_dt_bias"]),
               a=_pad_lanes(-jnp.exp(p["ssd_A_log"].astype(F32))),
               d=row(jnp.repeat(p["ssd_D"].astype(F32), SSD_HEAD_DIM)), ng=row(p["ssd_norm_g"]))
    post = dict(wso=bf(p["w_ssd_out"]), wfo=bf(p["w_fox_out"]), wo=bf(p["w_o"]), gpost=row(p["g_post_mix"]),
                gpre=row(p["g_pre_ffn"]))
    ffn = dict(wi=bf(p["w_ffn_in"]), cw=p["ffn_conv_w"].astype(F32), cb=row(p["ffn_conv_b"]),
               wout=bf(p["w_ffn_out"]), gpost=row(p["g_post_ffn"]))
    return dict(inproj=inproj, ssd=ssd, post=post, ffn=ffn, g_pre=row(p["g_pre_mix"]))


def _group_tiles(bt, t):
    if t >= 256:
        return 1, 256
    return max(1, min(bt, 128 // t)), t


def _trunk_layer(x, mod, ssd_buf, ssd_h0, ffn_buf, past, w):
    bt, t, d = x.shape
    sh1, sc1, g1, sh2, sc2, g2 = [m.reshape(bt, 1, d) for m in jnp.split(mod, 6, axis=-1)]
    nb, tt = _group_tiles(bt, t)
    z, xbc, dtr, q, k, v, lf, ga, gb = _inproj_call(x, sc1, sh1, w["g_pre"], w["inproj"], nb, tt)

    n_heads = ssd_h0.shape[1]
    h0p = ssd_h0.astype(F32).reshape(bt, n_heads // 2, 2 * SSD_HEAD_DIM, SSD_STATE)
    L = 128 if t % 128 == 0 else t
    ys, conv8, hfp = _ssd_call(xbc, z, dtr, _pad_hist(ssd_buf), h0p, w["ssd"], L)
    new_ssd_buf = conv8[:, CONV_PAD - (SSD_CONV - 1):, :]
    h_t = hfp.reshape(ssd_h0.shape)

    if past is None:
        f, ft = _fcum_call(lf, 512)
        o = _flash_call(q, k, v, f, ft, 256)
    else:
        cache_k, cache_v, cache_lf, page_table = past
        o = _paged_call(q, k, v, lf, cache_k, cache_v, cache_lf, page_table, 8)

    x1, h2 = _post_call(x, ys, o, ga, gb, g1, sc2, sh2, w["post"], nb, tt)
    y, ffn8 = _ffn_call(x1, h2, g2, _pad_hist(ffn_buf), w["ffn"], nb, tt)
    new_ffn_buf = ffn8[:, CONV_PAD - (FFN_CONV - 1):, :]

    n_kv = k.shape[2] // FOX_HEAD_DIM
    kh = k.reshape(bt, t, n_kv, FOX_HEAD_DIM)
    vh = v.reshape(bt, t, n_kv, FOX_HEAD_DIM)
    return y, (kh, vh, lf, new_ssd_buf, h_t, new_ffn_buf)


def kernel(x_prompt, x_sample, cache_k, cache_v, cache_logf, state_ssd_conv, state_ssd, state_ffn_conv,
           page_table, c_prompt, c_sample, w_ada, b_ada, g_pre_mix, w_in, ssd_conv_w, ssd_conv_b,
           ssd_dt_bias, ssd_A_log, ssd_D, ssd_norm_g, w_ssd_out, fox_b_f, w_fox_out, w_merge, b_merge,
           w_o, g_post_mix, g_pre_ffn, w_ffn_in, ffn_conv_w, ffn_conv_b, w_ffn_out, g_post_ffn):
    weights = dict(w_ada=w_ada, b_ada=b_ada, g_pre_mix=g_pre_mix, w_in=w_in, ssd_conv_w=ssd_conv_w,
                   ssd_conv_b=ssd_conv_b, ssd_dt_bias=ssd_dt_bias, ssd_A_log=ssd_A_log, ssd_D=ssd_D,
                   ssd_norm_g=ssd_norm_g, w_ssd_out=w_ssd_out, fox_b_f=fox_b_f, w_fox_out=w_fox_out,
                   w_merge=w_merge, b_merge=b_merge, w_o=w_o, g_post_mix=g_post_mix, g_pre_ffn=g_pre_ffn,
                   w_ffn_in=w_ffn_in, ffn_conv_w=ffn_conv_w, ffn_conv_b=ffn_conv_b, w_ffn_out=w_ffn_out,
                   g_post_ffn=g_post_ffn)
    depth = w_in.shape[0]
    bp = x_prompt.shape[0]
    n_pool, page = cache_k.shape[1], cache_k.shape[2]
    c_all = jnp.concatenate([c_prompt, c_sample], axis=0)
    yp, ys = x_prompt, x_sample
    outs_p = [[] for _ in range(6)]
    outs_s = [[] for _ in range(6)]
    for l in range(depth):
        p = {name: wt[l] for name, wt in weights.items()}
        w = _prep_weights(p)
        mod = _ada_call(c_all, p["w_ada"], p["b_ada"])
        ssd_buf0 = jnp.zeros((bp,) + state_ssd_conv.shape[2:], F32)
        ssd_h0 = jnp.zeros((bp,) + state_ssd.shape[2:], F32)
        ffn_buf0 = jnp.zeros((bp,) + state_ffn_conv.shape[2:], F32)
        yp, st_p = _trunk_layer(yp, mod[:bp], ssd_buf0, ssd_h0, ffn_buf0, None, w)
        past = (jnp.transpose(cache_k[l], (0, 2, 3, 1)).reshape(n_pool, -1, page),
                jnp.transpose(cache_v[l], (0, 2, 3, 1)).reshape(n_pool, -1, page),
                jnp.transpose(cache_logf[l], (0, 2, 1)), page_table)
        ys, st_s = _trunk_layer(ys, mod[bp:], state_ssd_conv[l], state_ssd[l], state_ffn_conv[l], past, w)
        for i in range(6):
            outs_p[i].append(st_p[i])
            outs_s[i].append(st_s[i])
    k_p, v_p, lf_p, sc_p, ss_p, fc_p = [jnp.stack(o) for o in outs_p]
    k_s, v_s, lf_s, sc_s, ss_s, fc_s = [jnp.stack(o) for o in outs_s]
    return (yp, ys, k_p, v_p, lf_p, sc_p, ss_p, fc_p, k_s, v_s, lf_s, sc_s, ss_s, fc_s)
```

```python
import functools

import jax
import jax.numpy as jnp
from jax import lax
from jax.experimental import pallas as pl
from jax.experimental.pallas import tpu as pltpu

F32 = jnp.float32
BF16 = jnp.bfloat16
EPS = 1e-6
NEG_INF = -1e30
HIGHEST = lax.Precision.HIGHEST

LANES = 128
SUBLANES = 8
VMEM_LIMIT = 56 * 1024 * 1024

SSD_HEAD_DIM = 64
SSD_GROUPS = 2
SSD_STATE = 128
SSD_CONV = 4
FOX_HEAD_DIM = 64
FOX_Q_PER_KV = 4
FFN_CONV = 3
CONV_PAD = 8
ATTN_BLOCK = 512
PAGES_PER_STEP = 16
BIAS_TERMS = 3
BIAS_ROWS = 16

NT_DIMS = (((1,), (1,)), ((), ()))
TN_DIMS = (((0,), (0,)), ((), ()))


def _dot(a, b):
    return jnp.dot(a, b, preferred_element_type=F32)


def _dot_nt(a, b, precision=None):
    return lax.dot_general(a, b, NT_DIMS, precision=precision, preferred_element_type=F32)


def _dot_tn(a, b):
    return lax.dot_general(a, b, TN_DIMS, preferred_element_type=F32)


def _dot_hi(a, b):
    return jnp.dot(a, b, precision=HIGHEST, preferred_element_type=F32)


def _sigmoid(x):
    return 1.0 / (1.0 + jnp.exp(-x))


def _silu(x):
    return x * _sigmoid(x)


def _softplus(x):
    return jnp.maximum(x, 0.0) + jnp.log(1.0 + jnp.exp(-jnp.abs(x)))


def _log_sigmoid(x):
    return jnp.minimum(x, 0.0) - jnp.log(1.0 + jnp.exp(-jnp.abs(x)))


def _gelu_tanh(x):
    c = 0.7978845608028654
    return 0.5 * x * (1.0 + jnp.tanh(c * (x + 0.044715 * (x * x * x))))


def _rms(x, g):
    ms = jnp.mean(x * x, axis=-1, keepdims=True)
    return x * lax.rsqrt(ms + EPS) * g


def _params(sem):
    return pltpu.CompilerParams(dimension_semantics=sem, vmem_limit_bytes=VMEM_LIMIT)


def _const_spec(shape):
    n = len(shape)
    return pl.BlockSpec(shape, lambda *_: (0,) * n, pipeline_mode=pl.Buffered(1))


def _ada_kernel(c_ref, w_ref, b_ref, o_ref):
    s = _silu(c_ref[...]).astype(BF16)
    o_ref[...] = _dot(s, w_ref[...].astype(BF16)) + b_ref[...]


def _ada_call(c_all, w_ada, b_ada):
    n, d = c_all.shape
    dn = w_ada.shape[1]
    tn = 1024
    return pl.pallas_call(
        _ada_kernel,
        grid=(dn // tn,),
        in_specs=[pl.BlockSpec((n, d), lambda j: (0, 0)),
                  pl.BlockSpec((d, tn), lambda j: (0, j)),
                  pl.BlockSpec((1, tn), lambda j: (0, j))],
        out_specs=pl.BlockSpec((n, tn), lambda j: (0, j)),
        out_shape=jax.ShapeDtypeStruct((n, dn), F32),
        compiler_params=_params(("arbitrary",)),
        name="ada",
    )(c_all, w_ada, b_ada.reshape(1, dn))


def _inproj_kernel(x_ref, sc_ref, sh_ref, g_ref, wz, wxbc, wdt, wq, wk, wv, wf, wmg, bmg, bf_ref,
                   z_o, xbc_o, dt_o, q_o, k_o, v_o, lf_o, ga_o, gb_o):
    nb, tt, d = x_ref.shape
    h = _rms(x_ref[...], g_ref[...]) * (1.0 + sc_ref[...]) + sh_ref[...]
    hb = h.reshape(nb * tt, d).astype(BF16)

    def proj(w):
        return _dot(hb, w[...])

    z_o[...] = proj(wz).reshape(z_o.shape)
    xbc_o[...] = proj(wxbc).reshape(xbc_o.shape)
    dt_o[...] = proj(wdt).reshape(dt_o.shape)
    q_o[...] = (proj(wq) * (FOX_HEAD_DIM ** -0.5)).astype(BF16).reshape(q_o.shape)
    k_o[...] = proj(wk).reshape(k_o.shape)
    v_o[...] = proj(wv).reshape(v_o.shape)
    nh = lf_o.shape[-1]
    lf = _log_sigmoid(proj(wf)[:, :nh] + bf_ref[...])
    lf_o[...] = lf.reshape(lf_o.shape)
    gates = _sigmoid(proj(wmg) + bmg[...])
    dm = ga_o.shape[-1]
    ga_o[...] = gates[:, :dm].reshape(ga_o.shape)
    gb_o[...] = gates[:, dm:].reshape(gb_o.shape)


def _inproj_call(x3, sc, sh, g_pre, w, nb, tt):
    bt, t, d = x3.shape
    grid = (bt // nb, t // tt)
    n_z, n_xbc, n_q, n_kv = w["wz"].shape[1], w["wxbc"].shape[1], w["wq"].shape[1], w["wk"].shape[1]
    nh = w["bf"].shape[1]
    dm = w["wmg"].shape[1] // 2

    def row_spec(n):
        return pl.BlockSpec((nb, tt, n), lambda i, j: (i, j, 0))

    mod_spec = pl.BlockSpec((nb, 1, d), lambda i, j: (i, 0, 0))
    consts = [g_pre, w["wz"], w["wxbc"], w["wdt"], w["wq"], w["wk"], w["wv"], w["wf"], w["wmg"], w["bmg"], w["bf"]]
    out_dims = [(n_z, F32), (n_xbc, F32), (LANES, F32), (n_q, BF16), (n_kv, F32), (n_kv, F32), (nh, F32),
                (dm, F32), (dm, F32)]
    return pl.pallas_call(
        _inproj_kernel,
        grid=grid,
        in_specs=[row_spec(d), mod_spec, mod_spec] + [_const_spec(c.shape) for c in consts],
        out_specs=[row_spec(n) for n, _ in out_dims],
        out_shape=[jax.ShapeDtypeStruct((bt, t, n), dt) for n, dt in out_dims],
        compiler_params=_params(("arbitrary", "arbitrary")),
        name="inproj",
    )(x3, sc, sh, *consts)


def _split3(x):
    hi = x.astype(BF16).astype(F32)
    r1 = x - hi
    mid = r1.astype(BF16).astype(F32)
    lo = (r1 - mid).astype(BF16).astype(F32)
    return hi, mid, lo


def _prep_kernel(q_ref, k_ref, v_ref, lf_ref, qa_o, kt_o, fb_o, vh_o, carry):
    t = lf_ref.shape[1]
    nh = lf_ref.shape[2]
    n_kv = vh_o.shape[1]

    @pl.when(pl.program_id(1) == 0)
    def _():
        carry[...] = jnp.zeros_like(carry)

    lf = lf_ref[0]
    r = lax.broadcasted_iota(jnp.int32, (t, t), 0)
    c = lax.broadcasted_iota(jnp.int32, (t, t), 1)
    f = _dot_hi((r >= c).astype(F32), lf) + carry[...]
    carry[...] = f[t - 1:t, :]
    eye = (lax.broadcasted_iota(jnp.int32, (nh, nh), 0) == lax.broadcasted_iota(jnp.int32, (nh, nh), 1)).astype(F32)
    f_t = _dot_nt(eye, f, precision=HIGHEST)
    fq = _split3(f)
    fk = _split3(f_t)

    q = q_ref[0]
    lane = lax.broadcasted_iota(jnp.int32, (t, FOX_HEAD_DIM), 1)
    sub = lax.broadcasted_iota(jnp.int32, (BIAS_ROWS, t), 0)
    for h in range(nh):
        ext = jnp.where(lane < 2 * BIAS_TERMS, 1.0, 0.0)
        for i in range(BIAS_TERMS):
            ext = jnp.where(lane == i, fq[i][:, h:h + 1], ext)
        qa_o[0, h] = jnp.concatenate([q[:, h * FOX_HEAD_DIM:(h + 1) * FOX_HEAD_DIM], ext.astype(BF16)], axis=1)
        kb = jnp.where(sub < BIAS_TERMS, 1.0, 0.0)
        for i in range(BIAS_TERMS):
            kb = jnp.where(sub == BIAS_TERMS + i, -fk[i][h:h + 1, :], kb)
        fb_o[0, h] = kb.astype(BF16)
    kt_o[0] = k_ref[0].T.astype(BF16)
    v = v_ref[0].astype(BF16)
    for g in range(n_kv):
        vh_o[0, g] = v[:, g * FOX_HEAD_DIM:(g + 1) * FOX_HEAD_DIM]


def _prep_call(q, k, v, lf, tt):
    b, t, nq = q.shape
    nkv = k.shape[2]
    nh = lf.shape[2]
    n_kv = nkv // FOX_HEAD_DIM
    return pl.pallas_call(
        _prep_kernel,
        grid=(b, t // tt),
        in_specs=[pl.BlockSpec((1, tt, nq), lambda i, j: (i, j, 0)),
                  pl.BlockSpec((1, tt, nkv), lambda i, j: (i, j, 0)),
                  pl.BlockSpec((1, tt, nkv), lambda i, j: (i, j, 0)),
                  pl.BlockSpec((1, tt, nh), lambda i, j: (i, j, 0))],
        out_specs=[pl.BlockSpec((1, nh, tt, 2 * FOX_HEAD_DIM), lambda i, j: (i, 0, j, 0)),
                   pl.BlockSpec((1, nkv, tt), lambda i, j: (i, 0, j)),
                   pl.BlockSpec((1, nh, BIAS_ROWS, tt), lambda i, j: (i, 0, 0, j)),
                   pl.BlockSpec((1, n_kv, tt, FOX_HEAD_DIM), lambda i, j: (i, 0, j, 0))],
        out_shape=[jax.ShapeDtypeStruct((b, nh, t, 2 * FOX_HEAD_DIM), BF16),
                   jax.ShapeDtypeStruct((b, nkv, t), BF16),
                   jax.ShapeDtypeStruct((b, nh, BIAS_ROWS, t), BF16),
                   jax.ShapeDtypeStruct((b, n_kv, t, FOX_HEAD_DIM), BF16)],
        scratch_shapes=[pltpu.VMEM((1, nh), F32)],
        compiler_params=_params(("arbitrary", "arbitrary")),
        name="prep",
    )(q, k, v, lf)


def _ssd_kernel(xbc_ref, z_ref, dt_ref, buf_ref, h0_ref, cw_ref, cb_ref, dtb_ref, a_ref, d_ref, ng_ref,
                y_o, cs_o, hf_o, xp, st):
    L = xbc_ref.shape[1]
    n_inner = z_ref.shape[2]
    n_state = SSD_STATE
    n_pairs = st.shape[0]
    pairs_per_group = n_pairs // SSD_GROUPS

    @pl.when(pl.program_id(1) == 0)
    def _():
        xp[0:CONV_PAD, :] = buf_ref[0]
        st[...] = h0_ref[0]

    xp[CONV_PAD:CONV_PAD + L, :] = xbc_ref[0]
    acc = cb_ref[...] + cw_ref[0:1, :] * xp[pl.ds(CONV_PAD - (SSD_CONV - 1), L), :]
    for j in range(1, SSD_CONV):
        acc = acc + cw_ref[j:j + 1, :] * xp[pl.ds(CONV_PAD - (SSD_CONV - 1) + j, L), :]
    tail = xp[pl.ds(L, CONV_PAD), :]
    xp[0:CONV_PAD, :] = tail
    cs_o[0] = tail
    xbc = _silu(acc)
    xs = xbc[:, :n_inner]

    dt = _softplus(dt_ref[0] + dtb_ref[...])
    a = dt * a_ref[...]
    r = lax.broadcasted_iota(jnp.int32, (L, L), 0)
    c = lax.broadcasted_iota(jnp.int32, (L, L), 1)
    causal = r >= c
    acum = _dot_hi(causal.astype(F32), a)
    eye = (lax.broadcasted_iota(jnp.int32, (LANES, LANES), 0)
           == lax.broadcasted_iota(jnp.int32, (LANES, LANES), 1)).astype(F32)
    a_t = _dot_nt(eye, a, precision=HIGHEST)
    dt_t = _dot_nt(eye, dt, precision=HIGHEST)
    acum_t = _dot_hi(a_t, (r <= c).astype(F32))
    atot = acum[L - 1:L, :]
    ea = jnp.exp(acum)
    de = jnp.exp(atot - acum) * dt

    lane = lax.broadcasted_iota(jnp.int32, (L, LANES), 1)
    first = lane < SSD_HEAD_DIM
    row_first = lax.broadcasted_iota(jnp.int32, (LANES, n_state), 0) < SSD_HEAD_DIM

    ys = []
    for g in range(SSD_GROUPS):
        bg = xbc[:, n_inner + g * n_state:n_inner + (g + 1) * n_state].astype(BF16)
        cg = xbc[:, n_inner + (SSD_GROUPS + g) * n_state:n_inner + (SSD_GROUPS + g + 1) * n_state].astype(BF16)
        cb = _dot_nt(cg, bg)
        for pp in range(pairs_per_group):
            pr = g * pairs_per_group + pp
            x_pair = xs[:, pr * LANES:(pr + 1) * LANES]
            xb = x_pair.astype(BF16)
            yd = []
            for hh in (2 * pr, 2 * pr + 1):
                seg = acum[:, hh:hh + 1] - acum_t[hh:hh + 1, :]
                decay = jnp.exp(jnp.where(causal, seg, -jnp.inf))
                m = cb * decay * dt_t[hh:hh + 1, :]
                yd.append(_dot(m.astype(BF16), xb))
            y_diag = jnp.where(first, yd[0], yd[1])
            s_pair = st[pr]
            y_off = _dot_nt(cg, s_pair.astype(BF16))
            h0, h1 = 2 * pr, 2 * pr + 1
            ea_pair = jnp.where(first, ea[:, h0:h0 + 1], ea[:, h1:h1 + 1])
            ys.append(y_diag + y_off * ea_pair)
            w_pair = jnp.where(first, de[:, h0:h0 + 1], de[:, h1:h1 + 1])
            s_add = _dot_tn((x_pair * w_pair).astype(BF16), bg)
            cd = jnp.exp(jnp.where(row_first, acum_t[h0:h0 + 1, L - 1:L], acum_t[h1:h1 + 1, L - 1:L]))
            st[pr] = cd * s_pair + s_add
    hf_o[0] = st[...]

    y = jnp.concatenate(ys, axis=1) + d_ref[...] * xs
    y = y * _silu(z_ref[0])
    gw = n_inner // SSD_GROUPS
    yn = []
    for g in range(SSD_GROUPS):
        yg = y[:, g * gw:(g + 1) * gw]
        yn.append(yg * lax.rsqrt(jnp.mean(yg * yg, axis=-1, keepdims=True) + EPS))
    y_o[0] = (jnp.concatenate(yn, axis=1) * ng_ref[...]).astype(BF16)


def _ssd_call(xbc, z, dtr, buf8, h0p, p, L):
    bt, t, n_xbc = xbc.shape
    n_inner = z.shape[2]
    n_pairs = h0p.shape[1]
    consts = [p["cw"], p["cb"], p["dtb"], p["a"], p["d"], p["ng"]]

    def seq_spec(n):
        return pl.BlockSpec((1, L, n), lambda i, j: (i, j, 0))

    return pl.pallas_call(
        _ssd_kernel,
        grid=(bt, t // L),
        in_specs=[seq_spec(n_xbc), seq_spec(n_inner), seq_spec(LANES),
                  pl.BlockSpec((1, CONV_PAD, n_xbc), lambda i, j: (i, 0, 0)),
                  pl.BlockSpec((1, n_pairs, LANES, SSD_STATE), lambda i, j: (i, 0, 0, 0))]
                 + [_const_spec(c.shape) for c in consts],
        out_specs=[seq_spec(n_inner),
                   pl.BlockSpec((1, CONV_PAD, n_xbc), lambda i, j: (i, 0, 0)),
                   pl.BlockSpec((1, n_pairs, LANES, SSD_STATE), lambda i, j: (i, 0, 0, 0))],
        out_shape=[jax.ShapeDtypeStruct((bt, t, n_inner), BF16),
                   jax.ShapeDtypeStruct((bt, CONV_PAD, n_xbc), F32),
                   jax.ShapeDtypeStruct(h0p.shape, F32)],
        scratch_shapes=[pltpu.VMEM((CONV_PAD + L, n_xbc), F32),
                        pltpu.VMEM((n_pairs, LANES, SSD_STATE), F32)],
        compiler_params=_params(("arbitrary", "arbitrary")),
        name="ssd",
    )(xbc, z, dtr, buf8, h0p, *consts)


def _flash_kernel(qa_ref, kt_ref, fb_ref, vh_ref, o_ref, m_s, l_s, acc_s):
    n_heads, tq, kdim = qa_ref.shape[1:]
    tk = kt_ref.shape[2]
    qi = pl.program_id(1)
    kj = pl.program_id(2)

    @pl.when(kj == 0)
    def _():
        m_s[...] = jnp.full_like(m_s, NEG_INF)
        l_s[...] = jnp.zeros_like(l_s)
        acc_s[...] = jnp.zeros_like(acc_s)

    def step(on_diagonal):
        if on_diagonal:
            visible = (lax.broadcasted_iota(jnp.int32, (tq, tk), 1)
                       <= lax.broadcasted_iota(jnp.int32, (tq, tk), 0))
        pad = jnp.zeros((kdim - FOX_HEAD_DIM - BIAS_ROWS, tk), BF16)
        for h in range(n_heads):
            g = h // FOX_Q_PER_KV
            k_ext = jnp.concatenate([kt_ref[0, g * FOX_HEAD_DIM:(g + 1) * FOX_HEAD_DIM, :], fb_ref[0, h], pad], axis=0)
            s = _dot(qa_ref[0, h], k_ext)
            if on_diagonal:
                s = jnp.where(visible, s, NEG_INF)
            m_old = m_s[h]
            m_new = jnp.maximum(m_old, jnp.max(s, axis=-1, keepdims=True))
            alpha = jnp.exp(m_old - m_new)
            p = jnp.exp(s - jnp.concatenate([m_new] * (tk // LANES), axis=1))
            l_s[h] = alpha * l_s[h] + jnp.sum(p, axis=-1, keepdims=True)
            m_s[h] = m_new
            acc_s[h] = alpha[:, :FOX_HEAD_DIM] * acc_s[h] + _dot(p.astype(BF16), vh_ref[0, g])

    @pl.when(kj < qi)
    def _():
        step(False)

    @pl.when(kj == qi)
    def _():
        step(True)
        outs = [acc_s[h] / l_s[h][:, :FOX_HEAD_DIM] for h in range(n_heads)]
        o_ref[0] = jnp.concatenate(outs, axis=1).astype(o_ref.dtype)


def _flash_call(qa, kt, fb, vh, tq):
    b, nh, t, kdim = qa.shape
    nkv = kt.shape[1]
    n_kv = vh.shape[1]
    nblk = t // tq

    def key_blk(i, a, c):
        return jnp.minimum(a, c)

    return pl.pallas_call(
        _flash_kernel,
        grid=(b, nblk, nblk),
        in_specs=[pl.BlockSpec((1, nh, tq, kdim), lambda i, a, c: (i, 0, a, 0)),
                  pl.BlockSpec((1, nkv, tq), lambda i, a, c: (i, 0, key_blk(i, a, c))),
                  pl.BlockSpec((1, nh, BIAS_ROWS, tq), lambda i, a, c: (i, 0, 0, key_blk(i, a, c))),
                  pl.BlockSpec((1, n_kv, tq, FOX_HEAD_DIM), lambda i, a, c: (i, 0, key_blk(i, a, c), 0))],
        out_specs=pl.BlockSpec((1, tq, nh * FOX_HEAD_DIM), lambda i, a, c: (i, a, 0)),
        out_shape=jax.ShapeDtypeStruct((b, t, nh * FOX_HEAD_DIM), BF16),
        scratch_shapes=[pltpu.VMEM((nh, tq, LANES), F32), pltpu.VMEM((nh, tq, LANES), F32),
                        pltpu.VMEM((nh, tq, FOX_HEAD_DIM), F32)],
        compiler_params=_params(("arbitrary", "arbitrary", "arbitrary")),
        name="flash",
    )(qa, kt, fb, vh)


def _paged_kernel(pt_ref, q_ref, kn_ref, vn_ref, lfn_ref, *rest, pages_per_step):
    del pt_ref
    P = pages_per_step
    k_refs, v_refs, lf_refs = rest[:P], rest[P:2 * P], rest[2 * P:3 * P]
    o_ref = rest[3 * P]
    qbd, qaug, knp, vnp, lfp, m_s, l_s, acc_s, carry = rest[3 * P + 1:]
    tq = q_ref.shape[1]
    n_heads = q_ref.shape[2] // FOX_HEAD_DIM
    n_kv = n_heads // FOX_Q_PER_KV
    rows = n_heads * tq
    kvw = n_kv * FOX_HEAD_DIM
    page = k_refs[0].shape[2]
    step = pl.program_id(1)

    row_head = lax.broadcasted_iota(jnp.int32, (rows, n_heads), 0) // tq
    expand = (row_head == lax.broadcasted_iota(jnp.int32, (rows, n_heads), 1)).astype(F32)
    pr = lax.broadcasted_iota(jnp.int32, (page, page), 0)
    pc = lax.broadcasted_iota(jnp.int32, (page, page), 1)

    @pl.when(step == 0)
    def _():
        q = q_ref[0].astype(F32)
        lane = lax.broadcasted_iota(jnp.int32, (tq, kvw), 1)
        blocks = []
        for h in range(n_heads):
            g, j = h // FOX_Q_PER_KV, h % FOX_Q_PER_KV
            qg = q[:, g * kvw:(g + 1) * kvw]
            shift = ((g - j) * FOX_HEAD_DIM) % kvw
            rolled = pltpu.roll(qg, shift, 1) if shift else qg
            blocks.append(jnp.where((lane >= g * FOX_HEAD_DIM) & (lane < (g + 1) * FOX_HEAD_DIM), rolled, 0.0))
        qbd[...] = jnp.concatenate(blocks, axis=0).astype(BF16)

        knp[...] = jnp.zeros_like(knp)
        vnp[...] = jnp.zeros_like(vnp)
        lfp[...] = jnp.zeros_like(lfp)
        knp[0:tq, :] = kn_ref[0]
        vnp[0:tq, :] = vn_ref[0]
        lfp[0:tq, :] = lfn_ref[0]
        le = _dot_nt(expand, lfp[...], precision=HIGHEST)
        cn = _dot_hi(le, (pr <= pc).astype(F32))
        col = lax.broadcasted_iota(jnp.int32, (rows, page), 1)
        qidx = lax.broadcasted_iota(jnp.int32, (rows, page), 0) % tq
        qn = jnp.sum(jnp.where(col == qidx, cn, 0.0), axis=-1, keepdims=True)
        s = _dot_nt(qbd[...], knp[...].astype(BF16)) + (qn - cn)
        s = jnp.where(col <= qidx, s, NEG_INF)
        m = jnp.max(s, axis=-1, keepdims=True)
        p = jnp.exp(s - m)
        m_s[...] = jnp.broadcast_to(m, m_s.shape)
        l_s[...] = jnp.broadcast_to(jnp.sum(p, axis=-1, keepdims=True), l_s.shape)
        acc_s[...] = _dot(p.astype(BF16), vnp[...].astype(BF16))
        ca = lax.broadcasted_iota(jnp.int32, qaug.shape, 1)
        ra = lax.broadcasted_iota(jnp.int32, qaug.shape, 0) // tq
        sel = jnp.where((ca < BIAS_TERMS * n_heads) & (ca % n_heads == ra), 1.0, 0.0)
        for i, part in enumerate(_split3(qn)):
            sel = jnp.where(ca == BIAS_TERMS * n_heads + i, part, sel)
        qaug[...] = sel.astype(BF16)
        carry[...] = jnp.zeros_like(carry)

    ph = P * n_heads
    lf_all = jnp.concatenate([lf_refs[i][0] for i in range(P)], axis=0)
    rloc = _dot_hi(lf_all, (pr > pc).astype(F32))
    tot = jnp.broadcast_to(jnp.sum(lf_all, axis=-1, keepdims=True), (ph, LANES))
    ri = lax.broadcasted_iota(jnp.int32, (ph, ph), 0)
    ci = lax.broadcasted_iota(jnp.int32, (ph, ph), 1)
    same_head = (ri % n_heads) == (ci % n_heads)
    newer = _dot_hi((same_head & (ci // n_heads < ri // n_heads)).astype(F32), tot)
    bias_parts = [b.astype(BF16) for b in _split3(rloc + newer + carry[...])]
    carry[...] = carry[...] + _dot_hi(same_head.astype(F32), tot)
    sub = lax.broadcasted_iota(jnp.int32, (n_heads, page), 0)
    ones_rows = jnp.where(sub < BIAS_TERMS, 1.0, 0.0).astype(BF16)
    scores = []
    for i in range(P):
        rows_i = slice(i * n_heads, (i + 1) * n_heads)
        b_ext = jnp.concatenate([b[rows_i] for b in bias_parts] + [ones_rows], axis=0)
        scores.append(_dot(qbd[...], k_refs[i][0].astype(BF16)) + _dot(qaug[...], b_ext))
    s = jnp.concatenate(scores, axis=1)
    m_old = m_s[...]
    m_new = jnp.maximum(m_old, jnp.max(s, axis=-1, keepdims=True))
    alpha = jnp.exp(m_old - m_new)
    pf = jnp.exp(s - jnp.concatenate([m_new] * P, axis=1))
    l_s[...] = alpha * l_s[...] + jnp.sum(pf, axis=-1, keepdims=True)
    m_s[...] = m_new
    p = pf.astype(BF16)
    pv = _dot_nt(p[:, 0:page], v_refs[0][0].astype(BF16))
    for i in range(1, P):
        pv = pv + _dot_nt(p[:, i * page:(i + 1) * page], v_refs[i][0].astype(BF16))
    acc_s[...] = jnp.concatenate([alpha] * (kvw // LANES), axis=1) * acc_s[...] + pv

    @pl.when(step == pl.num_programs(1) - 1)
    def _():
        o = acc_s[...] / jnp.concatenate([l_s[...]] * (kvw // LANES), axis=1)
        lane = lax.broadcasted_iota(jnp.int32, (tq, kvw), 1)
        groups = []
        for g in range(n_kv):
            tot = jnp.zeros((tq, kvw), F32)
            for j in range(FOX_Q_PER_KV):
                h = g * FOX_Q_PER_KV + j
                blk = o[h * tq:(h + 1) * tq, :]
                shift = ((j - g) * FOX_HEAD_DIM) % kvw
                rolled = pltpu.roll(blk, shift, 1) if shift else blk
                tot = tot + jnp.where((lane >= j * FOX_HEAD_DIM) & (lane < (j + 1) * FOX_HEAD_DIM), rolled, 0.0)
            groups.append(tot)
        o_ref[0] = jnp.concatenate(groups, axis=1).astype(o_ref.dtype)


def _paged_call(q, kn, vn, lfn, cache_k, cache_v, cache_lf, page_table, pages_per_step):
    b, tq, nq = q.shape
    n_pool, kvw, page = cache_k.shape
    nh = cache_lf.shape[1]
    n_pages = page_table.shape[1]
    P = pages_per_step
    steps = n_pages // P
    rows = nh * tq

    def new_spec(n):
        return pl.BlockSpec((1, tq, n), lambda i, s, pt: (i, 0, 0))

    def page_spec(n, slot):
        return pl.BlockSpec((1, n, page), lambda i, s, pt: (pt[i, n_pages - 1 - (s * P + slot)], 0, 0))

    grid_spec = pltpu.PrefetchScalarGridSpec(
        num_scalar_prefetch=1,
        grid=(b, steps),
        in_specs=[new_spec(nq), new_spec(kvw), new_spec(kvw), new_spec(nh)]
                 + [page_spec(kvw, i) for i in range(P)]
                 + [page_spec(kvw, i) for i in range(P)]
                 + [page_spec(nh, i) for i in range(P)],
        out_specs=pl.BlockSpec((1, tq, nq), lambda i, s, pt: (i, 0, 0)),
        scratch_shapes=[pltpu.VMEM((rows, kvw), BF16), pltpu.VMEM((rows, (BIAS_TERMS + 1) * nh), BF16),
                        pltpu.VMEM((page, kvw), F32), pltpu.VMEM((page, kvw), F32), pltpu.VMEM((page, nh), F32),
                        pltpu.VMEM((rows, LANES), F32), pltpu.VMEM((rows, LANES), F32),
                        pltpu.VMEM((rows, kvw), F32),
                        pltpu.VMEM((P * nh, LANES), F32)],
    )
    return pl.pallas_call(
        functools.partial(_paged_kernel, pages_per_step=P),
        grid_spec=grid_spec,
        out_shape=jax.ShapeDtypeStruct((b, tq, nq), BF16),
        compiler_params=_params(("arbitrary", "arbitrary")),
        name="paged",
    )(page_table, q, kn, vn, lfn, *([cache_k] * P), *([cache_v] * P), *([cache_lf] * P))


def _post_kernel(x_ref, ys_ref, o_ref, ga_ref, gb_ref, g1_ref, sc2_ref, sh2_ref,
                 wso, wfo, wo, gpost, gpre, x1_o, h2_o):
    nb, tt, d = x_ref.shape
    rows = nb * tt
    y_ssd = _dot(ys_ref[...].reshape(rows, -1), wso[...])
    y_fox = _dot(o_ref[...].reshape(rows, -1), wfo[...])
    mix_in = ga_ref[...].reshape(rows, d) * y_ssd + gb_ref[...].reshape(rows, d) * y_fox
    mix = _dot(mix_in.astype(BF16), wo[...]).reshape(nb, tt, d)
    x1 = x_ref[...] + g1_ref[...] * _rms(mix, gpost[...])
    x1_o[...] = x1
    h2 = _rms(x1, gpre[...]) * (1.0 + sc2_ref[...]) + sh2_ref[...]
    h2_o[...] = h2.astype(BF16)


def _post_call(x3, ys, o, ga, gb, g1, sc2, sh2, w, nb, tt):
    bt, t, d = x3.shape
    n_inner, n_fox = ys.shape[2], o.shape[2]

    def row_spec(n):
        return pl.BlockSpec((nb, tt, n), lambda i, j: (i, j, 0))

    mod_spec = pl.BlockSpec((nb, 1, d), lambda i, j: (i, 0, 0))
    consts = [w["wso"], w["wfo"], w["wo"], w["gpost"], w["gpre"]]
    return pl.pallas_call(
        _post_kernel,
        grid=(bt // nb, t // tt),
        in_specs=[row_spec(d), row_spec(n_inner), row_spec(n_fox), row_spec(d), row_spec(d),
                  mod_spec, mod_spec, mod_spec] + [_const_spec(c.shape) for c in consts],
        out_specs=[row_spec(d), row_spec(d)],
        out_shape=[jax.ShapeDtypeStruct((bt, t, d), F32), jax.ShapeDtypeStruct((bt, t, d), BF16)],
        compiler_params=_params(("arbitrary", "arbitrary")),
        name="post",
    )(x3, ys, o, ga, gb, g1, sc2, sh2, *consts)


def _ffn_kernel(x1_ref, h2_ref, g2_ref, buf_ref, wi, cw_ref, cb_ref, wout, gpost, y_o, cs_o, xp):
    nb, tt, d = x1_ref.shape
    dff = cw_ref.shape[1]
    rows = nb * tt

    @pl.when(pl.program_id(1) == 0)
    def _():
        xp[:, 0:CONV_PAD, :] = buf_ref[...]

    ug = _dot(h2_ref[...].reshape(rows, d), wi[...])
    xp[:, CONV_PAD:CONV_PAD + tt, :] = ug[:, :dff].reshape(nb, tt, dff)
    conv = cb_ref[...] + cw_ref[0:1, :] * xp[:, pl.ds(CONV_PAD - (FFN_CONV - 1), tt), :]
    for j in range(1, FFN_CONV):
        conv = conv + cw_ref[j:j + 1, :] * xp[:, pl.ds(CONV_PAD - (FFN_CONV - 1) + j, tt), :]
    tail = xp[:, pl.ds(tt, CONV_PAD), :]
    xp[:, 0:CONV_PAD, :] = tail
    cs_o[...] = tail
    act = _gelu_tanh(conv).reshape(rows, dff) * ug[:, dff:]
    f = _dot(act.astype(BF16), wout[...]).reshape(nb, tt, d)
    y_o[...] = x1_ref[...] + g2_ref[...] * _rms(f, gpost[...])


def _ffn_call(x1, h2, g2, buf8, w, nb, tt):
    bt, t, d = x1.shape
    dff = w["cw"].shape[1]

    def row_spec(n):
        return pl.BlockSpec((nb, tt, n), lambda i, j: (i, j, 0))

    consts = [w["wi"], w["cw"], w["cb"], w["wout"], w["gpost"]]
    return pl.pallas_call(
        _ffn_kernel,
        grid=(bt // nb, t // tt),
        in_specs=[row_spec(d), row_spec(d), pl.BlockSpec((nb, 1, d), lambda i, j: (i, 0, 0)),
                  pl.BlockSpec((nb, CONV_PAD, dff), lambda i, j: (i, 0, 0))]
                 + [_const_spec(c.shape) for c in consts],
        out_specs=[row_spec(d), pl.BlockSpec((nb, CONV_PAD, dff), lambda i, j: (i, 0, 0))],
        out_shape=[jax.ShapeDtypeStruct((bt, t, d), F32), jax.ShapeDtypeStruct((bt, CONV_PAD, dff), F32)],
        scratch_shapes=[pltpu.VMEM((nb, CONV_PAD + tt, dff), F32)],
        compiler_params=_params(("arbitrary", "arbitrary")),
        name="ffn",
    )(x1, h2, g2, buf8, *consts)


def _pad_hist(buf):
    return jnp.pad(buf, ((0, 0), (CONV_PAD - buf.shape[1], 0), (0, 0)))


def _pad_lanes(v):
    return jnp.pad(v.astype(F32), (0, LANES - v.shape[0])).reshape(1, LANES)


def _prep_weights(p):
    d = p["w_in"].shape[0]
    n_heads = p["ssd_dt_bias"].shape[0]
    n_inner = n_heads * SSD_HEAD_DIM
    n_gn = SSD_GROUPS * SSD_STATE
    nf = p["fox_b_f"].shape[0]
    n_fox = nf * FOX_HEAD_DIM
    n_kv = n_fox // FOX_Q_PER_KV
    cuts = [n_inner, n_inner + 2 * n_gn, n_heads, n_fox, n_kv, n_kv, nf]
    offs = [0]
    for c in cuts:
        offs.append(offs[-1] + c)
    w_in = p["w_in"]
    cols = [w_in[:, offs[i]:offs[i + 1]] for i in range(len(cuts))]

    def pad_cols(w):
        return jnp.pad(w, ((0, 0), (0, LANES - w.shape[1])))

    bf = lambda w: w.astype(BF16)
    row = lambda v: v.astype(F32).reshape(1, -1)
    inproj = dict(wz=bf(cols[0]), wxbc=bf(cols[1]), wdt=bf(pad_cols(cols[2])), wq=bf(cols[3]), wk=bf(cols[4]),
                  wv=bf(cols[5]), wf=bf(pad_cols(cols[6])), wmg=bf(p["w_merge"]), bmg=row(p["b_merge"]),
                  bf=row(p["fox_b_f"]))
    ssd = dict(cw=p["ssd_conv_w"].astype(F32), cb=row(p["ssd_conv_b"]), dtb=_pad_lanes(p["ssd_dt_bias"]),
               a=_pad_lanes(-jnp.exp(p["ssd_A_log"].astype(F32))),
               d=row(jnp.repeat(p["ssd_D"].astype(F32), SSD_HEAD_DIM)), ng=row(p["ssd_norm_g"]))
    post = dict(wso=bf(p["w_ssd_out"]), wfo=bf(p["w_fox_out"]), wo=bf(p["w_o"]), gpost=row(p["g_post_mix"]),
                gpre=row(p["g_pre_ffn"]))
    ffn = dict(wi=bf(p["w_ffn_in"]), cw=p["ffn_conv_w"].astype(F32), cb=row(p["ffn_conv_b"]),
               wout=bf(p["w_ffn_out"]), gpost=row(p["g_post_ffn"]))
    return dict(inproj=inproj, ssd=ssd, post=post, ffn=ffn, g_pre=row(p["g_pre_mix"]))


def _group_tiles(bt, t):
    if t >= 256:
        return 1, 256
    return max(1, min(bt, 128 // t)), t


def _trunk_layer(x, mod, ssd_buf, ssd_h0, ffn_buf, past, w):
    bt, t, d = x.shape
    sh1, sc1, g1, sh2, sc2, g2 = [m.reshape(bt, 1, d) for m in jnp.split(mod, 6, axis=-1)]
    nb, tt = _group_tiles(bt, t)
    z, xbc, dtr, q, k, v, lf, ga, gb = _inproj_call(x, sc1, sh1, w["g_pre"], w["inproj"], nb, tt)

    n_heads = ssd_h0.shape[1]
    h0p = ssd_h0.astype(F32).reshape(bt, n_heads // 2, 2 * SSD_HEAD_DIM, SSD_STATE)
    L = 128 if t % 128 == 0 else t
    ys, conv8, hfp = _ssd_call(xbc, z, dtr, _pad_hist(ssd_buf), h0p, w["ssd"], L)
    new_ssd_buf = conv8[:, CONV_PAD - (SSD_CONV - 1):, :]
    h_t = hfp.reshape(ssd_h0.shape)

    if past is None:
        qa, kt, fb, vhm = _prep_call(q, k, v, lf, ATTN_BLOCK)
        o = _flash_call(qa, kt, fb, vhm, ATTN_BLOCK)
    else:
        cache_k, cache_v, cache_lf, page_table = past
        o = _paged_call(q, k, v, lf, cache_k, cache_v, cache_lf, page_table, PAGES_PER_STEP)

    x1, h2 = _post_call(x, ys, o, ga, gb, g1, sc2, sh2, w["post"], nb, tt)
    y, ffn8 = _ffn_call(x1, h2, g2, _pad_hist(ffn_buf), w["ffn"], nb, tt)
    new_ffn_buf = ffn8[:, CONV_PAD - (FFN_CONV - 1):, :]

    n_kv = k.shape[2] // FOX_HEAD_DIM
    kh = k.reshape(bt, t, n_kv, FOX_HEAD_DIM)
    vh = v.reshape(bt, t, n_kv, FOX_HEAD_DIM)
    return y, (kh, vh, lf, new_ssd_buf, h_t, new_ffn_buf)


def kernel(x_prompt, x_sample, cache_k, cache_v, cache_logf, state_ssd_conv, state_ssd, state_ffn_conv,
           page_table, c_prompt, c_sample, w_ada, b_ada, g_pre_mix, w_in, ssd_conv_w, ssd_conv_b,
           ssd_dt_bias, ssd_A_log, ssd_D, ssd_norm_g, w_ssd_out, fox_b_f, w_fox_out, w_merge, b_merge,
           w_o, g_post_mix, g_pre_ffn, w_ffn_in, ffn_conv_w, ffn_conv_b, w_ffn_out, g_post_ffn):
    weights = dict(w_ada=w_ada, b_ada=b_ada, g_pre_mix=g_pre_mix, w_in=w_in, ssd_conv_w=ssd_conv_w,
                   ssd_conv_b=ssd_conv_b, ssd_dt_bias=ssd_dt_bias, ssd_A_log=ssd_A_log, ssd_D=ssd_D,
                   ssd_norm_g=ssd_norm_g, w_ssd_out=w_ssd_out, fox_b_f=fox_b_f, w_fox_out=w_fox_out,
                   w_merge=w_merge, b_merge=b_merge, w_o=w_o, g_post_mix=g_post_mix, g_pre_ffn=g_pre_ffn,
                   w_ffn_in=w_ffn_in, ffn_conv_w=ffn_conv_w, ffn_conv_b=ffn_conv_b, w_ffn_out=w_ffn_out,
                   g_post_ffn=g_post_ffn)
    depth = w_in.shape[0]
    bp = x_prompt.shape[0]
    n_pool, page = cache_k.shape[1], cache_k.shape[2]
    c_all = jnp.concatenate([c_prompt, c_sample], axis=0)
    yp, ys = x_prompt, x_sample
    outs_p = [[] for _ in range(6)]
    outs_s = [[] for _ in range(6)]
    for l in range(depth):
        p = {name: wt[l] for name, wt in weights.items()}
        w = _prep_weights(p)
        mod = _ada_call(c_all, p["w_ada"], p["b_ada"])
        ssd_buf0 = jnp.zeros((bp,) + state_ssd_conv.shape[2:], F32)
        ssd_h0 = jnp.zeros((bp,) + state_ssd.shape[2:], F32)
        ffn_buf0 = jnp.zeros((bp,) + state_ffn_conv.shape[2:], F32)
        yp, st_p = _trunk_layer(yp, mod[:bp], ssd_buf0, ssd_h0, ffn_buf0, None, w)
        past = (jnp.transpose(cache_k[l], (0, 2, 3, 1)).reshape(n_pool, -1, page),
                jnp.transpose(cache_v[l], (0, 2, 3, 1)).reshape(n_pool, -1, page),
                jnp.transpose(cache_logf[l], (0, 2, 1)), page_table)
        ys, st_s = _trunk_layer(ys, mod[bp:], state_ssd_conv[l], state_ssd[l], state_ffn_conv[l], past, w)
        for i in range(6):
            outs_p[i].append(st_p[i])
            outs_s[i].append(st_s[i])
    k_p, v_p, lf_p, sc_p, ss_p, fc_p = [jnp.stack(o) for o in outs_p]
    k_s, v_s, lf_s, sc_s, ss_s, fc_s = [jnp.stack(o) for o in outs_s]
    return (yp, ys, k_p, v_p, lf_p, sc_p, ss_p, fc_p, k_s, v_s, lf_s, sc_s, ss_s, fc_s)
```
